```python
import math
import jax
import jax.numpy as jnp
from jax import lax
import numpy as np

D_MODEL = 1024
BATCH = 4
SEQ = 8192
DEPTH = 4

N_MEM = 256
N_MIXERS = 3
EPS = 1e-6

D_MIX = D_MODEL
XA_HEADS = 4
XA_HEAD_DIM = 128
D_XA = XA_HEADS * XA_HEAD_DIM
D_CAT = D_MIX + D_XA
D_GATE = D_CAT

GM_CHUNK = 128
GM_GROUPS = 8
GM_GROUP_DIM = D_MIX // GM_GROUPS

SC_WIDTH = 3

DN_HEADS = 8
DN_HEAD_DIM = D_MIX // DN_HEADS
DN_CONV = 4
DN_CHUNK = 64

N_LAYERS_A = (DEPTH + 2) // N_MIXERS
N_LAYERS_B = (DEPTH + 1) // N_MIXERS
N_LAYERS_C = DEPTH // N_MIXERS

A_IN = 2 * D_MIX + D_XA + D_GATE
B_IN = 3 * D_MIX + D_XA + D_GATE
C_IN = 3 * D_MIX + 2 * DN_HEADS + D_XA + D_GATE

kernel_name = "hybrid_gmlp_shortconv_gdn_memattn_trunk"


def rmsnorm(x, w):
    xf = x.astype(jnp.float32)
    y = xf * lax.rsqrt(jnp.mean(xf * xf, axis=-1, keepdims=True) + EPS)
    return (y * w.astype(jnp.float32)).astype(x.dtype)


def layernorm(x, w, b):
    xf = x.astype(jnp.float32)
    xc = xf - jnp.mean(xf, axis=-1, keepdims=True)
    y = xc * lax.rsqrt(jnp.mean(xc * xc, axis=-1, keepdims=True) + EPS)
    return (y * w.astype(jnp.float32) + b.astype(jnp.float32)).astype(x.dtype)


def l2norm(x):
    xf = x.astype(jnp.float32)
    return xf * lax.rsqrt(jnp.sum(xf * xf, axis=-1, keepdims=True) + EPS)


def causal_depthwise_conv(x, w):
    K = w.shape[0]
    S = x.shape[1]
    xp = jnp.pad(x, ((0, 0), (K - 1, 0), (0, 0)))
    y = xp[:, 0:S] * w[0]
    for k in range(1, K):
        y = y + xp[:, k:k + S] * w[k]
    return y


def memory_attention(q, mem_k, mem_v):
    B, S, _ = q.shape
    qh = q.reshape(B, S, XA_HEADS, XA_HEAD_DIM)
    s = jnp.einsum("bshd,bmhd->bhsm", qh, mem_k).astype(jnp.float32) * (XA_HEAD_DIM ** -0.5)
    p = jax.nn.softmax(s, axis=-1).astype(q.dtype)
    o = jnp.einsum("bhsm,bmhd->bshd", p, mem_v)
    return o.reshape(B, S, D_XA)


def chunked_causal_gmlp(u, v, ln_w, ln_b, w_s, b_s):
    B, S, _ = u.shape
    n_chunks = S // GM_CHUNK
    u = jax.nn.gelu(u)
    v = layernorm(jax.nn.gelu(v), ln_w, ln_b)
    vc = v.reshape(B, n_chunks, GM_CHUNK, GM_GROUPS, GM_GROUP_DIM)
    causal = jnp.tril(jnp.ones((GM_CHUNK, GM_CHUNK), dtype=bool))
    ws = jnp.where(causal[None], w_s, jnp.zeros_like(w_s))
    sp = jnp.einsum("gts,bnsgc->bntgc", ws, vc) + b_s.T[:, :, None]
    return u * sp.reshape(B, S, D_MIX)


def chunk_gated_delta_rule(q, k, v, g, beta):
    B, S, H, D = q.shape
    C = DN_CHUNK
    N = S // C

    def to_chunks(t):
        t = t.reshape((B, N, C, H) + t.shape[3:])
        return jnp.moveaxis(t, 3, 1)

    qc, kc, vc = to_chunks(q), to_chunks(k), to_chunks(v)
    gc, bc = to_chunks(g), to_chunks(beta)
    gcum = jnp.cumsum(gc, axis=-1)
    incl = jnp.tril(jnp.ones((C, C), dtype=bool))
    strict = jnp.tril(jnp.ones((C, C), dtype=bool), k=-1)
    decay = jnp.exp(jnp.where(incl, gcum[..., :, None] - gcum[..., None, :], -jnp.inf))
    kb = kc * bc[..., None]
    a_mat = jnp.where(strict, jnp.einsum("bhnid,bhnjd->bhnij", kb, kc) * decay, 0.0)
    eye = jnp.eye(C, dtype=q.dtype)
    t_mat = lax.linalg.triangular_solve(eye + a_mat, jnp.broadcast_to(eye, a_mat.shape),
                                        left_side=True, lower=True, unit_diagonal=True)
    u_c = jnp.einsum("bhnij,bhnjd->bhnid", t_mat, vc * bc[..., None])
    w_c = jnp.einsum("bhnij,bhnjd->bhnid", t_mat, kb * jnp.exp(gcum)[..., None])
    qk = jnp.einsum("bhnid,bhnjd->bhnij", qc, kc) * decay
    q_dec = qc * jnp.exp(gcum)[..., None]
    k_dec = kc * jnp.exp(gcum[..., -1:] - gcum)[..., None]
    g_last = jnp.exp(gcum[..., -1])
    xs = tuple(jnp.moveaxis(t, 2, 0) for t in (q_dec, qk, u_c, w_c, k_dec, g_last))

    def step(state, inp):
        q_i, qk_i, u_i, w_i, k_i, gl_i = inp
        v_new = u_i - jnp.einsum("bhcd,bhde->bhce", w_i, state)
        o_i = (jnp.einsum("bhcd,bhde->bhce", q_i, state)
               + jnp.einsum("bhij,bhje->bhie", qk_i, v_new))
        state = state * gl_i[..., None, None] + jnp.einsum("bhcd,bhce->bhde", k_i, v_new)
        return state, o_i

    state0 = jnp.zeros((B, H, D, D), dtype=q.dtype)
    _, o = lax.scan(step, state0, xs)
    o = jnp.transpose(o, (1, 0, 3, 2, 4))
    return o.reshape(B, S, H, D)


def gated_deltanet(qkv, a, b, conv_w, a_log, dt_bias, o_norm_w):
    B, S, _ = qkv.shape
    dtype = qkv.dtype
    qkv = jax.nn.silu(causal_depthwise_conv(qkv, conv_w))
    q, k, v = jnp.split(qkv, 3, axis=-1)
    shp = (B, S, DN_HEADS, DN_HEAD_DIM)
    q = l2norm(q.reshape(shp)) * (DN_HEAD_DIM ** -0.5)
    k = l2norm(k.reshape(shp))
    v = v.reshape(shp).astype(jnp.float32)
    beta = jax.nn.sigmoid(b.astype(jnp.float32))
    g = -jnp.exp(a_log.astype(jnp.float32)) * jax.nn.softplus(
        a.astype(jnp.float32) + dt_bias.astype(jnp.float32))
    o = chunk_gated_delta_rule(q, k, v, g, beta)
    o = rmsnorm(o, o_norm_w)
    return o.reshape(B, S, D_MIX).astype(dtype)


def branch_a(h, w_in, ln_w, ln_b, w_s, b_s, mem_k, mem_v):
    u, v, qx, z = jnp.split(h @ w_in, [D_MIX, 2 * D_MIX, 2 * D_MIX + D_XA], axis=-1)
    y = chunked_causal_gmlp(u, v, ln_w, ln_b, w_s, b_s)
    return jnp.concatenate([y, memory_attention(qx, mem_k, mem_v)], axis=-1) * jax.nn.silu(z)


def branch_b(h, w_in, conv_w, mem_k, mem_v):
    bg, cg, hv, qx, z = jnp.split(h @ w_in, [D_MIX, 2 * D_MIX, 3 * D_MIX, 3 * D_MIX + D_XA], axis=-1)
    y = bg * causal_depthwise_conv(cg * hv, conv_w)
    return jnp.concatenate([y, memory_attention(qx, mem_k, mem_v)], axis=-1) * jax.nn.silu(z)


def branch_c(h, w_in, conv_w, a_log, dt_bias, o_norm_w, mem_k, mem_v):
    c0 = 3 * D_MIX
    qkv, a, b, qx, z = jnp.split(
        h @ w_in, [c0, c0 + DN_HEADS, c0 + 2 * DN_HEADS, c0 + 2 * DN_HEADS + D_XA], axis=-1)
    y = gated_deltanet(qkv, a, b, conv_w, a_log, dt_bias, o_norm_w)
    return jnp.concatenate([y, memory_attention(qx, mem_k, mem_v)], axis=-1) * jax.nn.silu(z)


def setup_inputs(seed: int = 0) -> dict:
    key = jax.random.key(seed)
    ks = jax.random.split(key, 20)
    f32 = jnp.float32

    def normal(k, shape, scale):
        return jax.random.normal(k, shape, f32) * scale

    def gain(k, shape):
        return 1.0 + 0.02 * jax.random.normal(k, shape, f32)

    a_coef = jax.random.uniform(ks[17], (N_LAYERS_C, DN_HEADS), f32, 1.0, 16.0)
    dt = jnp.exp(jax.random.uniform(ks[18], (N_LAYERS_C, DN_HEADS), f32,
                                    math.log(1e-3), math.log(1e-1)))
    return {
        "x": normal(ks[0], (BATCH, SEQ, D_MODEL), 1.0),
        "mem": normal(ks[1], (BATCH, N_MEM, D_MODEL), 1.0),
        "mem_norm_w": gain(ks[2], (D_MODEL,)),
        "w_mem_kv": normal(ks[3], (D_MODEL, 2 * D_XA), D_MODEL ** -0.5),
        "norm_pre": gain(ks[4], (DEPTH, D_MODEL)),
        "norm_post": gain(ks[5], (DEPTH, D_MODEL)),
        "w_out": normal(ks[6], (DEPTH, D_CAT, D_MODEL), D_CAT ** -0.5),
        "a_w_in": normal(ks[7], (N_LAYERS_A, D_MODEL, A_IN), D_MODEL ** -0.5),
        "a_ln_w": gain(ks[8], (N_LAYERS_A, D_MIX)),
        "a_ln_b": normal(ks[9], (N_LAYERS_A, D_MIX), 0.02),
        "a_w_s": normal(ks[10], (N_LAYERS_A, GM_GROUPS, GM_CHUNK, GM_CHUNK), 0.5 * GM_CHUNK ** -0.5),
        "a_b_s": gain(ks[11], (N_LAYERS_A, GM_GROUPS, GM_CHUNK)),
        "b_w_in": normal(ks[12], (N_LAYERS_B, D_MODEL, B_IN), D_MODEL ** -0.5),
        "b_conv_w": normal(ks[13], (N_LAYERS_B, SC_WIDTH, D_MIX), SC_WIDTH ** -0.5),
        "c_w_in": normal(ks[14], (N_LAYERS_C, D_MODEL, C_IN), D_MODEL ** -0.5),
        "c_conv_w": normal(ks[15], (N_LAYERS_C, DN_CONV, 3 * D_MIX), DN_CONV ** -0.5),
        "c_a_log": jnp.log(a_coef),
        "c_dt_bias": dt + jnp.log(-jnp.expm1(-dt)),
        "c_o_norm_w": gain(ks[16], (N_LAYERS_C, DN_HEAD_DIM)),
    }


def reference(x, mem, mem_norm_w, w_mem_kv, norm_pre, norm_post, w_out,
              a_w_in, a_ln_w, a_ln_b, a_w_s, a_b_s,
              b_w_in, b_conv_w,
              c_w_in, c_conv_w, c_a_log, c_dt_bias, c_o_norm_w):
    B = mem.shape[0]
    mem_k, mem_v = jnp.split(rmsnorm(mem, mem_norm_w) @ w_mem_kv, 2, axis=-1)
    mem_k = mem_k.reshape(B, N_MEM, XA_HEADS, XA_HEAD_DIM)
    mem_v = mem_v.reshape(B, N_MEM, XA_HEADS, XA_HEAD_DIM)
    for i in range(DEPTH):
        kind, j = i % N_MIXERS, i // N_MIXERS
        h = rmsnorm(x, norm_pre[i])
        if kind == 0:
            y = branch_a(h, a_w_in[j], a_ln_w[j], a_ln_b[j], a_w_s[j], a_b_s[j], mem_k, mem_v)
        elif kind == 1:
            y = branch_b(h, b_w_in[j], b_conv_w[j], mem_k, mem_v)
        else:
            y = branch_c(h, c_w_in[j], c_conv_w[j], c_a_log[j], c_dt_bias[j], c_o_norm_w[j],
                         mem_k, mem_v)
        x = x + rmsnorm(y @ w_out[i], norm_post[i])
    return x
```

```python
import functools
import math

import jax
import jax.numpy as jnp
from jax import lax
from jax.experimental import pallas as pl
from jax.experimental.pallas import tpu as pltpu

_F32 = jnp.float32
_BF16 = jnp.bfloat16

_EPS = 1e-6
_XA_HEADS = 4
_HEAD = 128
_GM_CHUNK = 128
_DN_CHUNK = 64
_DN_PAIR = 2 * _DN_CHUNK
_DN_CONV = 4
_SC_WIDTH = 3
_CARRY_ROWS = 8

_TM_AB = 512
_TM_C = 256
_VMEM_LIMIT = 56 * 1024 * 1024


def _dot(a, b):
    return jnp.dot(a, b, preferred_element_type=_F32)


def _dot_nt(a, b):
    return lax.dot_general(a, b, (((1,), (1,)), ((), ())), preferred_element_type=_F32)


def _rmsnorm(x, w):
    ms = jnp.mean(x * x, axis=-1, keepdims=True)
    return x * lax.rsqrt(ms + _EPS) * w


def _sigmoid(x):
    return 1.0 / (1.0 + jnp.exp(-x))


def _silu(x):
    return x * _sigmoid(x)


def _gelu_tanh(x):
    c = math.sqrt(2.0 / math.pi)
    return 0.5 * x * (1.0 + jnp.tanh(c * (x + 0.044715 * (x * x * x))))


def _softplus(x):
    return jnp.maximum(x, 0.0) + jnp.log(1.0 + jnp.exp(-jnp.abs(x)))


def _shift_rows(x, carry, shift):
    rolled = pltpu.roll(x, shift, axis=0)
    row = lax.broadcasted_iota(jnp.int32, (x.shape[0], 1), 0)
    for r in range(shift):
        src = _CARRY_ROWS - shift + r
        rolled = jnp.where(row == r, carry[src:src + 1, :], rolled)
    return rolled


def _mem_attention(qx, kt_ref, v_ref):
    scale = _HEAD ** -0.5
    outs = []
    for h in range(_XA_HEADS):
        sl = slice(h * _HEAD, (h + 1) * _HEAD)
        s = _dot(qx[:, sl].astype(_BF16), kt_ref[0, sl, :]) * scale
        m = jnp.max(s, axis=-1, keepdims=True)
        p = jnp.exp(s - m)
        l = jnp.sum(p, axis=-1, keepdims=True)
        o = _dot(p.astype(_BF16), v_ref[0, :, sl])
        outs.append(o * (1.0 / l))
    return jnp.concatenate(outs, axis=-1)


def _gate_project_residual(x, mix, attn, z, wout_ref, npost):
    d_mix = mix.shape[-1]
    gate = _silu(z)
    y_mix = (mix * gate[:, :d_mix]).astype(_BF16)
    y_att = (attn * gate[:, d_mix:]).astype(_BF16)
    o = _dot(y_mix, wout_ref[:d_mix, :]) + _dot(y_att, wout_ref[d_mix:, :])
    return x + _rmsnorm(o, npost)


def _memkv_kernel(mem_ref, nw_ref, w_ref, kt_ref, v_ref):
    d_xa = v_ref.shape[-1]
    mn = _rmsnorm(mem_ref[0], nw_ref[...]).astype(_BF16)
    kv = _dot(mn, w_ref[...])
    kt_ref[0] = kv[:, :d_xa].T.astype(_BF16)
    v_ref[0] = kv[:, d_xa:].astype(_BF16)


def _memkv(mem, mem_norm_w, w_mem_kv):
    b, n_mem, d = mem.shape
    d_xa = w_mem_kv.shape[1] // 2
    return pl.pallas_call(
        _memkv_kernel,
        grid=(b,),
        in_specs=[
            pl.BlockSpec((1, n_mem, d), lambda i: (i, 0, 0)),
            pl.BlockSpec((1, d), lambda i: (0, 0)),
            pl.BlockSpec((d, 2 * d_xa), lambda i: (0, 0)),
        ],
        out_specs=[
            pl.BlockSpec((1, d_xa, n_mem), lambda i: (i, 0, 0)),
            pl.BlockSpec((1, n_mem, d_xa), lambda i: (i, 0, 0)),
        ],
        out_shape=[
            jax.ShapeDtypeStruct((b, d_xa, n_mem), _BF16),
            jax.ShapeDtypeStruct((b, n_mem, d_xa), _BF16),
        ],
        compiler_params=pltpu.CompilerParams(dimension_semantics=("arbitrary",)),
        name="memkv",
    )(mem, mem_norm_w.reshape(1, d), w_mem_kv.astype(_BF16))


def _layer_a_kernel(x_ref, npre_ref, win_ref, lnw_ref, lnb_ref, ws_ref, bs_ref,
                    kt_ref, v_ref, wout_ref, npost_ref, o_ref):
    x = x_ref[0]
    tm, d = x.shape
    d_xa = v_ref.shape[-1]
    h = _rmsnorm(x, npre_ref[...]).astype(_BF16)

    u = _gelu_tanh(_dot(h, win_ref[:, 0:d]))
    v = _gelu_tanh(_dot(h, win_ref[:, d:2 * d]))
    vc = v - jnp.mean(v, axis=-1, keepdims=True)
    v = vc * lax.rsqrt(jnp.mean(vc * vc, axis=-1, keepdims=True) + _EPS)
    v = (v * lnw_ref[...] + lnb_ref[...]).astype(_BF16)

    n_groups = ws_ref.shape[0]
    ti = lax.broadcasted_iota(jnp.int32, (_GM_CHUNK, _GM_CHUNK), 0)
    si = lax.broadcasted_iota(jnp.int32, (_GM_CHUNK, _GM_CHUNK), 1)
    causal = ti >= si
    ws = [jnp.where(causal, ws_ref[g], 0.0).astype(_BF16) for g in range(n_groups)]
    sp_rows = []
    for n in range(tm // _GM_CHUNK):
        rs = slice(n * _GM_CHUNK, (n + 1) * _GM_CHUNK)
        sp_rows.append(jnp.concatenate(
            [_dot(ws[g], v[rs, g * _HEAD:(g + 1) * _HEAD]) for g in range(n_groups)],
            axis=-1) + bs_ref[...])
    mix = u * jnp.concatenate(sp_rows, axis=0)

    attn = _mem_attention(_dot(h, win_ref[:, 2 * d:2 * d + d_xa]), kt_ref, v_ref)
    z = _dot(h, win_ref[:, 2 * d + d_xa:])
    o_ref[0] = _gate_project_residual(x, mix, attn, z, wout_ref, npost_ref[...])


def _layer_b_kernel(x_ref, npre_ref, win_ref, cw_ref, kt_ref, v_ref, wout_ref,
                    npost_ref, o_ref, carry_ref):
    @pl.when(pl.program_id(1) == 0)
    def _():
        carry_ref[...] = jnp.zeros_like(carry_ref)

    x = x_ref[0]
    tm, d = x.shape
    d_xa = v_ref.shape[-1]
    h = _rmsnorm(x, npre_ref[...]).astype(_BF16)

    bg = _dot(h, win_ref[:, 0:d])
    p = _dot(h, win_ref[:, d:2 * d]) * _dot(h, win_ref[:, 2 * d:3 * d])
    carry = carry_ref[...]
    conv = p * cw_ref[_SC_WIDTH - 1:_SC_WIDTH, :]
    for k in range(_SC_WIDTH - 1):
        shift = _SC_WIDTH - 1 - k
        conv = conv + _shift_rows(p, carry, shift) * cw_ref[k:k + 1, :]
    carry_ref[...] = p[tm - _CARRY_ROWS:, :]
    mix = bg * conv

    attn = _mem_attention(_dot(h, win_ref[:, 3 * d:3 * d + d_xa]), kt_ref, v_ref)
    z = _dot(h, win_ref[:, 3 * d + d_xa:])
    o_ref[0] = _gate_project_residual(x, mix, attn, z, wout_ref, npost_ref[...])


def _unit_lower_inverse(a):
    n = a.shape[0]
    ri = lax.broadcasted_iota(jnp.int32, (n, n), 0)
    ci = lax.broadcasted_iota(jnp.int32, (n, n), 1)
    same = ri ^ ci
    in16 = same < 16
    in32 = same < 32
    eye = (ri == ci).astype(_F32)

    def mm(p, q):
        return _dot(p.astype(_BF16), q.astype(_BF16))

    nd = jnp.where(in16, -a, 0.0)
    n2 = mm(nd, nd)
    x = eye + nd
    x = x + mm(x, n2)
    n4 = mm(n2, n2)
    x = x + mm(x, n4)
    n8 = mm(n4, n4)
    x = x + mm(x, n8)
    off32 = jnp.where(jnp.logical_and(in32, jnp.logical_not(in16)), a, 0.0)
    x = x - mm(mm(x, off32), x)
    off64 = jnp.where(in32, 0.0, a)
    x = x - mm(mm(x, off64), x)
    return x


def _layer_c_kernel(x_ref, npre_ref, wqkv_ref, wab_ref, wqz_ref, cw_ref, alog_ref,
                    dtb_ref, onw_ref, kt_ref, v_ref, wout_ref, npost_ref, o_ref,
                    carry_ref, state_ref):
    @pl.when(pl.program_id(1) == 0)
    def _():
        carry_ref[...] = jnp.zeros_like(carry_ref)
        state_ref[...] = jnp.zeros_like(state_ref)

    x = x_ref[0]
    tm, d = x.shape
    d_xa = v_ref.shape[-1]
    n_heads = state_ref.shape[0]
    h = _rmsnorm(x, npre_ref[...]).astype(_BF16)

    pre = _dot(h, wqkv_ref[...])
    carry = carry_ref[...]
    conv = pre * cw_ref[_DN_CONV - 1:_DN_CONV, :]
    for k in range(_DN_CONV - 1):
        shift = _DN_CONV - 1 - k
        conv = conv + _shift_rows(pre, carry, shift) * cw_ref[k:k + 1, :]
    carry_ref[...] = pre[tm - _CARRY_ROWS:, :]
    qkv = _silu(conv)

    ab = _dot(h, wab_ref[...])
    g = -jnp.exp(alog_ref[...]) * _softplus(ab[:, :_HEAD] + dtb_ref[...])
    beta = _sigmoid(ab[:, _HEAD:])
    t_in = lax.broadcasted_iota(jnp.int32, (tm, 1), 0) & (_DN_CHUNK - 1)
    gc = g
    step = 1
    while step < _DN_CHUNK:
        gc = gc + jnp.where(t_in >= step, pltpu.roll(gc, step, axis=0), 0.0)
        step *= 2
    gc3 = gc.reshape(tm // _DN_CHUNK, _DN_CHUNK, _HEAD)
    g_end = jnp.broadcast_to(gc3[:, _DN_CHUNK - 1:, :], gc3.shape).reshape(tm, _HEAD)
    gc_t = gc.T
    e_gc = jnp.exp(gc)
    e_rest = jnp.exp(g_end - gc)
    e_end = jnp.exp(g_end)

    ri = lax.broadcasted_iota(jnp.int32, (_DN_PAIR, _DN_PAIR), 0)
    ci = lax.broadcasted_iota(jnp.int32, (_DN_PAIR, _DN_PAIR), 1)
    incl = jnp.logical_and(ri >= ci, (ri ^ ci) < _DN_CHUNK)
    diag = ri == ci
    zeros_half = jnp.zeros((_DN_CHUNK, _HEAD), _F32)
    q_scale = _HEAD ** -0.5

    mix_heads = []
    for hd in range(n_heads):
        q = qkv[:, hd * _HEAD:(hd + 1) * _HEAD]
        k = qkv[:, d + hd * _HEAD:d + (hd + 1) * _HEAD]
        v = qkv[:, 2 * d + hd * _HEAD:2 * d + (hd + 1) * _HEAD]
        q = q * (lax.rsqrt(jnp.sum(q * q, axis=-1, keepdims=True) + _EPS) * q_scale)
        k = k * lax.rsqrt(jnp.sum(k * k, axis=-1, keepdims=True) + _EPS)
        beta_h = beta[:, hd:hd + 1]
        e_gc_h = e_gc[:, hd:hd + 1]
        kb = k * beta_h
        k_bf = k.astype(_BF16)
        rhs_uw = jnp.concatenate([v * beta_h, kb * e_gc_h], axis=-1).astype(_BF16)
        q_dec = q * e_gc_h
        kdec_t = (k * e_rest[:, hd:hd + 1]).T.astype(_BF16)
        kb_bf = kb.astype(_BF16)
        q_bf = q.astype(_BF16)
        e_end_h = e_end[:, hd:hd + 1]

        u_blocks, w_blocks, qk_blocks = [], [], []
        for pb in range(tm // _DN_PAIR):
            rs = slice(pb * _DN_PAIR, (pb + 1) * _DN_PAIR)
            diff = gc[rs, hd:hd + 1] - gc_t[hd:hd + 1, rs]
            decay = jnp.where(incl, jnp.exp(diff), 0.0)
            kq = _dot_nt(jnp.concatenate([kb_bf[rs, :], q_bf[rs, :]], axis=0), k_bf[rs, :])
            kk = kq[:_DN_PAIR, :]
            qk = kq[_DN_PAIR:, :]
            t_inv = _unit_lower_inverse(jnp.where(diag, 0.0, kk * decay))
            uw = _dot(t_inv.astype(_BF16), rhs_uw[rs, :])
            u_blocks.append(uw[:, :_HEAD])
            w_blocks.append(uw[:, _HEAD:])
            qk_blocks.append((qk * decay).astype(_BF16))

        s = state_ref[hd]
        o_chunks = []
        for c in range(tm // _DN_CHUNK):
            pb, half = divmod(c, 2)
            lo = half * _DN_CHUNK
            rs = slice(c * _DN_CHUNK, (c + 1) * _DN_CHUNK)
            w_c = w_blocks[pb][lo:lo + _DN_CHUNK, :]
            u_c = u_blocks[pb][lo:lo + _DN_CHUNK, :]
            lhs = jnp.concatenate([w_c, q_dec[rs, :]], axis=0).astype(_BF16)
            ws = _dot(lhs, s.astype(_BF16))
            v_new = u_c - ws[:_DN_CHUNK, :]
            halves = [zeros_half, zeros_half]
            halves[half] = v_new
            vn_pad = jnp.concatenate(halves, axis=0).astype(_BF16)
            o_chunks.append(ws[_DN_CHUNK:, :]
                            + _dot(qk_blocks[pb][lo:lo + _DN_CHUNK, :], vn_pad))
            prs = slice(pb * _DN_PAIR, (pb + 1) * _DN_PAIR)
            s_decay = jnp.concatenate([e_end_h[rs, :], e_end_h[rs, :]], axis=0)
            s = s * s_decay + _dot(kdec_t[:, prs], vn_pad)
        state_ref[hd] = s
        o_h = jnp.concatenate(o_chunks, axis=0)
        mix_heads.append(_rmsnorm(o_h, onw_ref[...]))
    mix = jnp.concatenate(mix_heads, axis=-1)

    attn = _mem_attention(_dot(h, wqz_ref[:, :d_xa]), kt_ref, v_ref)
    z = _dot(h, wqz_ref[:, d_xa:])
    o_ref[0] = _gate_project_residual(x, mix, attn, z, wout_ref, npost_ref[...])


def _const_spec(shape):
    zeros = (0,) * len(shape)
    return pl.BlockSpec(shape, lambda b, j: zeros, pipeline_mode=pl.Buffered(1))


def _layer_call(kernel, name, tm, x, consts, kt, v, scratch_shapes=()):
    b, s, d = x.shape
    lead, tail = consts[:-2], consts[-2:]
    operands = [x, *lead, kt, v, *tail]
    in_specs = [pl.BlockSpec((1, tm, d), lambda bi, j: (bi, j, 0))]
    in_specs += [_const_spec(c.shape) for c in lead]
    in_specs += [pl.BlockSpec((1,) + kt.shape[1:], lambda bi, j: (bi, 0, 0)),
                 pl.BlockSpec((1,) + v.shape[1:], lambda bi, j: (bi, 0, 0))]
    in_specs += [_const_spec(c.shape) for c in tail]
    return pl.pallas_call(
        kernel,
        grid=(b, s // tm),
        in_specs=in_specs,
        out_specs=pl.BlockSpec((1, tm, d), lambda bi, j: (bi, j, 0)),
        out_shape=jax.ShapeDtypeStruct(x.shape, x.dtype),
        scratch_shapes=list(scratch_shapes),
        compiler_params=pltpu.CompilerParams(
            dimension_semantics=("arbitrary", "arbitrary"),
            vmem_limit_bytes=_VMEM_LIMIT),
        name=name,
    )(*operands)


def kernel(x, mem, mem_norm_w, w_mem_kv, norm_pre, norm_post, w_out, a_w_in, a_ln_w, a_ln_b, a_w_s, a_b_s, b_w_in, b_conv_w, c_w_in, c_conv_w, c_a_log, c_dt_bias, c_o_norm_w):
    depth, d = norm_pre.shape
    n_mixers = 3
    d_xa = w_mem_kv.shape[1] // 2
    n_heads = c_a_log.shape[1]
    assert x.shape[1] % _TM_AB == 0 and x.shape[1] % _TM_C == 0
    assert d // n_heads == _HEAD and d_xa == _XA_HEADS * _HEAD

    kt, v = _memkv(mem, mem_norm_w, w_mem_kv)
    row = lambda p: p.reshape(1, -1)
    pad_lanes = lambda p: jnp.pad(p, ((0, 0), (0, _HEAD - p.shape[1])))

    for i in range(depth):
        kind, j = i % n_mixers, i // n_mixers
        wout = w_out[i].astype(_BF16)
        if kind == 0:
            bs_full = jnp.repeat(a_b_s[j].T, _HEAD, axis=1)
            consts = [row(norm_pre[i]), a_w_in[j].astype(_BF16), row(a_ln_w[j]),
                      row(a_ln_b[j]), a_w_s[j], bs_full, wout, row(norm_post[i])]
            x = _layer_call(_layer_a_kernel, "layer_gmlp", _TM_AB, x, consts, kt, v)
        elif kind == 1:
            consts = [row(norm_pre[i]), b_w_in[j].astype(_BF16), b_conv_w[j],
                      wout, row(norm_post[i])]
            x = _layer_call(_layer_b_kernel, "layer_sconv", _TM_AB, x, consts, kt, v,
                            scratch_shapes=[pltpu.VMEM((_CARRY_ROWS, d), _F32)])
        else:
            w = c_w_in[j]
            c0 = 3 * d
            w_ab = jnp.concatenate([pad_lanes(w[:, c0:c0 + n_heads]),
                                    pad_lanes(w[:, c0 + n_heads:c0 + 2 * n_heads])], axis=1)
            consts = [row(norm_pre[i]), w[:, :c0].astype(_BF16), w_ab.astype(_BF16),
                      w[:, c0 + 2 * n_heads:].astype(_BF16), c_conv_w[j],
                      pad_lanes(row(c_a_log[j])), pad_lanes(row(c_dt_bias[j])),
                      row(c_o_norm_w[j]), wout, row(norm_post[i])]
            x = _layer_call(_layer_c_kernel, "layer_gdn", _TM_C, x, consts, kt, v,
                            scratch_shapes=[pltpu.VMEM((_CARRY_ROWS, 3 * d), _F32),
                                            pltpu.VMEM((n_heads, _HEAD, _HEAD), _F32)])
    return x
```

```python
import functools
import math

import jax
import jax.numpy as jnp
from jax import lax
from jax.experimental import pallas as pl
from jax.experimental.pallas import tpu as pltpu

_F32 = jnp.float32
_BF16 = jnp.bfloat16

_EPS = 1e-6
_XA_HEADS = 4
_HEAD = 128
_GM_CHUNK = 128
_DN_CHUNK = 64
_DN_PAIR = 2 * _DN_CHUNK
_DN_CONV = 4
_SC_WIDTH = 3
_CARRY_ROWS = 8

_TM_AB = 512
_TM_C = 256
_VMEM_LIMIT = 56 * 1024 * 1024


def _dot(a, b):
    return jnp.dot(a, b, preferred_element_type=_F32)


def _dot_nt(a, b):
    return lax.dot_general(a, b, (((1,), (1,)), ((), ())), preferred_element_type=_F32)


def _rmsnorm(x, w):
    ms = jnp.mean(x * x, axis=-1, keepdims=True)
    return x * lax.rsqrt(ms + _EPS) * w


def _sigmoid(x):
    return 1.0 / (1.0 + jnp.exp(-x))


def _silu(x):
    return x * _sigmoid(x)


def _gelu_tanh(x):
    c = math.sqrt(2.0 / math.pi)
    return 0.5 * x * (1.0 + jnp.tanh(c * (x + 0.044715 * (x * x * x))))


def _softplus(x):
    return jnp.maximum(x, 0.0) + jnp.log(1.0 + jnp.exp(-jnp.abs(x)))


def _shift_rows(x, carry, shift):
    rolled = pltpu.roll(x, shift, axis=0)
    row = lax.broadcasted_iota(jnp.int32, (x.shape[0], 1), 0)
    for r in range(shift):
        src = _CARRY_ROWS - shift + r
        rolled = jnp.where(row == r, carry[src:src + 1, :], rolled)
    return rolled


def _mem_attention(qx, kt_ref, v_ref):
    scale = _HEAD ** -0.5
    outs = []
    for h in range(_XA_HEADS):
        sl = slice(h * _HEAD, (h + 1) * _HEAD)
        s = _dot(qx[:, sl].astype(_BF16), kt_ref[0, sl, :]) * scale
        m = jnp.max(s, axis=-1, keepdims=True)
        p = jnp.exp(s - m)
        l = jnp.sum(p, axis=-1, keepdims=True)
        o = _dot(p.astype(_BF16), v_ref[0, :, sl])
        outs.append(o * (1.0 / l))
    return jnp.concatenate(outs, axis=-1)


def _gate_project_residual(x, mix, attn, z, wout_ref, npost):
    d_mix = mix.shape[-1]
    gate = _silu(z)
    y_mix = (mix * gate[:, :d_mix]).astype(_BF16)
    y_att = (attn * gate[:, d_mix:]).astype(_BF16)
    o = _dot(y_mix, wout_ref[:d_mix, :]) + _dot(y_att, wout_ref[d_mix:, :])
    return x + _rmsnorm(o, npost)


def _memkv_kernel(mem_ref, nw_ref, w_ref, kt_ref, v_ref):
    d_xa = v_ref.shape[-1]
    mn = _rmsnorm(mem_ref[0], nw_ref[...]).astype(_BF16)
    kv = _dot(mn, w_ref[...])
    kt_ref[0] = kv[:, :d_xa].T.astype(_BF16)
    v_ref[0] = kv[:, d_xa:].astype(_BF16)


def _memkv(mem, mem_norm_w, w_mem_kv):
    b, n_mem, d = mem.shape
    d_xa = w_mem_kv.shape[1] // 2
    return pl.pallas_call(
        _memkv_kernel,
        grid=(b,),
        in_specs=[
            pl.BlockSpec((1, n_mem, d), lambda i: (i, 0, 0)),
            pl.BlockSpec((1, d), lambda i: (0, 0)),
            pl.BlockSpec((d, 2 * d_xa), lambda i: (0, 0)),
        ],
        out_specs=[
            pl.BlockSpec((1, d_xa, n_mem), lambda i: (i, 0, 0)),
            pl.BlockSpec((1, n_mem, d_xa), lambda i: (i, 0, 0)),
        ],
        out_shape=[
            jax.ShapeDtypeStruct((b, d_xa, n_mem), _BF16),
            jax.ShapeDtypeStruct((b, n_mem, d_xa), _BF16),
        ],
        compiler_params=pltpu.CompilerParams(dimension_semantics=("arbitrary",)),
        name="memkv",
    )(mem, mem_norm_w.reshape(1, d), w_mem_kv.astype(_BF16))


def _layer_a_kernel(x_ref, npre_ref, win_ref, lnw_ref, lnb_ref, ws_ref, bs_ref,
                    kt_ref, v_ref, wout_ref, npost_ref, o_ref):
    x = x_ref[0]
    tm, d = x.shape
    d_xa = v_ref.shape[-1]
    h = _rmsnorm(x, npre_ref[...]).astype(_BF16)

    u = _gelu_tanh(_dot(h, win_ref[:, 0:d]))
    v = _gelu_tanh(_dot(h, win_ref[:, d:2 * d]))
    vc = v - jnp.mean(v, axis=-1, keepdims=True)
    v = vc * lax.rsqrt(jnp.mean(vc * vc, axis=-1, keepdims=True) + _EPS)
    v = (v * lnw_ref[...] + lnb_ref[...]).astype(_BF16)

    n_groups = ws_ref.shape[0]
    ti = lax.broadcasted_iota(jnp.int32, (_GM_CHUNK, _GM_CHUNK), 0)
    si = lax.broadcasted_iota(jnp.int32, (_GM_CHUNK, _GM_CHUNK), 1)
    causal = ti >= si
    ws = [jnp.where(causal, ws_ref[g], 0.0).astype(_BF16) for g in range(n_groups)]
    sp_rows = []
    for n in range(tm // _GM_CHUNK):
        rs = slice(n * _GM_CHUNK, (n + 1) * _GM_CHUNK)
        sp_rows.append(jnp.concatenate(
            [_dot(ws[g], v[rs, g * _HEAD:(g + 1) * _HEAD]) for g in range(n_groups)],
            axis=-1) + bs_ref[...])
    mix = u * jnp.concatenate(sp_rows, axis=0)

    attn = _mem_attention(_dot(h, win_ref[:, 2 * d:2 * d + d_xa]), kt_ref, v_ref)
    z = _dot(h, win_ref[:, 2 * d + d_xa:])
    o_ref[0] = _gate_project_residual(x, mix, attn, z, wout_ref, npost_ref[...])


def _layer_b_kernel(x_ref, npre_ref, win_ref, cw_ref, kt_ref, v_ref, wout_ref,
                    npost_ref, o_ref, carry_ref):
    @pl.when(pl.program_id(1) == 0)
    def _():
        carry_ref[...] = jnp.zeros_like(carry_ref)

    x = x_ref[0]
    tm, d = x.shape
    d_xa = v_ref.shape[-1]
    h = _rmsnorm(x, npre_ref[...]).astype(_BF16)

    bg = _dot(h, win_ref[:, 0:d])
    p = _dot(h, win_ref[:, d:2 * d]) * _dot(h, win_ref[:, 2 * d:3 * d])
    carry = carry_ref[...]
    conv = p * cw_ref[_SC_WIDTH - 1:_SC_WIDTH, :]
    for k in range(_SC_WIDTH - 1):
        shift = _SC_WIDTH - 1 - k
        conv = conv + _shift_rows(p, carry, shift) * cw_ref[k:k + 1, :]
    carry_ref[...] = p[tm - _CARRY_ROWS:, :]
    mix = bg * conv

    attn = _mem_attention(_dot(h, win_ref[:, 3 * d:3 * d + d_xa]), kt_ref, v_ref)
    z = _dot(h, win_ref[:, 3 * d + d_xa:])
    o_ref[0] = _gate_project_residual(x, mix, attn, z, wout_ref, npost_ref[...])


def _unit_lower_inverse(mats):
    n = mats[0].shape[0]
    ri = lax.broadcasted_iota(jnp.int32, (n, n), 0)
    ci = lax.broadcasted_iota(jnp.int32, (n, n), 1)
    same = ri ^ ci
    in16 = same < 16
    in32 = same < 32
    mid32 = jnp.logical_and(in32, jnp.logical_not(in16))
    eye = (ri == ci).astype(_F32)

    def mm(ps, qs):
        return [_dot(p.astype(_BF16), q.astype(_BF16)) for p, q in zip(ps, qs)]

    def add(ps, qs):
        return [p + q for p, q in zip(ps, qs)]

    def sub(ps, qs):
        return [p - q for p, q in zip(ps, qs)]

    nd = [jnp.where(in16, -a, 0.0) for a in mats]
    n2 = mm(nd, nd)
    x = [eye + m for m in nd]
    x = add(x, mm(x, n2))
    n4 = mm(n2, n2)
    x = add(x, mm(x, n4))
    n8 = mm(n4, n4)
    x = add(x, mm(x, n8))
    off32 = [jnp.where(mid32, a, 0.0) for a in mats]
    x = sub(x, mm(mm(x, off32), x))
    off64 = [jnp.where(in32, 0.0, a) for a in mats]
    x = sub(x, mm(mm(x, off64), x))
    return x


def _layer_c_kernel(x_ref, npre_ref, wqkv_ref, wab_ref, wqz_ref, cw_ref, alog_ref,
                    dtb_ref, onw_ref, kt_ref, v_ref, wout_ref, npost_ref, o_ref,
                    carry_ref, state_ref):
    @pl.when(pl.program_id(1) == 0)
    def _():
        carry_ref[...] = jnp.zeros_like(carry_ref)
        state_ref[...] = jnp.zeros_like(state_ref)

    x = x_ref[0]
    tm, d = x.shape
    d_xa = v_ref.shape[-1]
    n_heads = state_ref.shape[0]
    h = _rmsnorm(x, npre_ref[...]).astype(_BF16)

    pre = _dot(h, wqkv_ref[...])
    carry = carry_ref[...]
    conv = pre * cw_ref[_DN_CONV - 1:_DN_CONV, :]
    for k in range(_DN_CONV - 1):
        shift = _DN_CONV - 1 - k
        conv = conv + _shift_rows(pre, carry, shift) * cw_ref[k:k + 1, :]
    carry_ref[...] = pre[tm - _CARRY_ROWS:, :]
    qkv = _silu(conv)

    ab = _dot(h, wab_ref[...])
    g = -jnp.exp(alog_ref[...]) * _softplus(ab[:, :_HEAD] + dtb_ref[...])
    beta = _sigmoid(ab[:, _HEAD:])
    t_in = lax.broadcasted_iota(jnp.int32, (tm, 1), 0) & (_DN_CHUNK - 1)
    gc = g
    step = 1
    while step < _DN_CHUNK:
        gc = gc + jnp.where(t_in >= step, pltpu.roll(gc, step, axis=0), 0.0)
        step *= 2
    gc3 = gc.reshape(tm // _DN_CHUNK, _DN_CHUNK, _HEAD)
    g_end = jnp.broadcast_to(gc3[:, _DN_CHUNK - 1:, :], gc3.shape).reshape(tm, _HEAD)
    gc_t = gc.T
    e_gc = jnp.exp(gc)
    e_rest = jnp.exp(g_end - gc)
    e_end = jnp.exp(g_end)

    ri = lax.broadcasted_iota(jnp.int32, (_DN_PAIR, _DN_PAIR), 0)
    ci = lax.broadcasted_iota(jnp.int32, (_DN_PAIR, _DN_PAIR), 1)
    incl = jnp.logical_and(ri >= ci, (ri ^ ci) < _DN_CHUNK)
    diag = ri == ci
    zeros_half = jnp.zeros((_DN_CHUNK, _HEAD), _F32)
    q_scale = _HEAD ** -0.5

    heads = range(n_heads)
    pairs = range(tm // _DN_PAIR)
    k_bf, kb_bf, q_bf, rhs_uw, q_dec, kdec_t, e_end_h = [], [], [], [], [], [], []
    for hd in heads:
        q = qkv[:, hd * _HEAD:(hd + 1) * _HEAD]
        k = qkv[:, d + hd * _HEAD:d + (hd + 1) * _HEAD]
        v = qkv[:, 2 * d + hd * _HEAD:2 * d + (hd + 1) * _HEAD]
        q = q * (lax.rsqrt(jnp.sum(q * q, axis=-1, keepdims=True) + _EPS) * q_scale)
        k = k * lax.rsqrt(jnp.sum(k * k, axis=-1, keepdims=True) + _EPS)
        beta_h = beta[:, hd:hd + 1]
        e_gc_h = e_gc[:, hd:hd + 1]
        kb = k * beta_h
        k_bf.append(k.astype(_BF16))
        kb_bf.append(kb.astype(_BF16))
        q_bf.append(q.astype(_BF16))
        rhs_uw.append(jnp.concatenate([v * beta_h, kb * e_gc_h], axis=-1).astype(_BF16))
        q_dec.append(q * e_gc_h)
        kdec_t.append((k * e_rest[:, hd:hd + 1]).T.astype(_BF16))
        e_end_h.append(e_end[:, hd:hd + 1])

    systems = [(hd, pb) for hd in heads for pb in pairs]
    prows = lambda pb: slice(pb * _DN_PAIR, (pb + 1) * _DN_PAIR)
    decay = [jnp.where(incl, jnp.exp(gc[prows(pb), hd:hd + 1] - gc_t[hd:hd + 1, prows(pb)]), 0.0)
             for hd, pb in systems]
    kq = [_dot_nt(jnp.concatenate([kb_bf[hd][prows(pb), :], q_bf[hd][prows(pb), :]], axis=0),
                  k_bf[hd][prows(pb), :]) for hd, pb in systems]
    t_inv = _unit_lower_inverse(
        [jnp.where(diag, 0.0, m[:_DN_PAIR, :] * dc) for m, dc in zip(kq, decay)])
    uw = {sys: _dot(t.astype(_BF16), rhs_uw[sys[0]][prows(sys[1]), :])
          for sys, t in zip(systems, t_inv)}
    qk = {sys: (m[_DN_PAIR:, :] * dc).astype(_BF16) for sys, m, dc in zip(systems, kq, decay)}

    s = [state_ref[hd] for hd in heads]
    o_chunks = [[] for _ in heads]
    for c in range(tm // _DN_CHUNK):
        pb, half = divmod(c, 2)
        lo = half * _DN_CHUNK
        rs = slice(c * _DN_CHUNK, (c + 1) * _DN_CHUNK)
        ws = [_dot(jnp.concatenate([uw[hd, pb][lo:lo + _DN_CHUNK, _HEAD:], q_dec[hd][rs, :]],
                                   axis=0).astype(_BF16), s[hd].astype(_BF16)) for hd in heads]
        vn_pad = []
        for hd in heads:
            halves = [zeros_half, zeros_half]
            halves[half] = uw[hd, pb][lo:lo + _DN_CHUNK, :_HEAD] - ws[hd][:_DN_CHUNK, :]
            vn_pad.append(jnp.concatenate(halves, axis=0).astype(_BF16))
        for hd in heads:
            o_chunks[hd].append(ws[hd][_DN_CHUNK:, :]
                                + _dot(qk[hd, pb][lo:lo + _DN_CHUNK, :], vn_pad[hd]))
        for hd in heads:
            s_decay = jnp.concatenate([e_end_h[hd][rs, :], e_end_h[hd][rs, :]], axis=0)
            s[hd] = s[hd] * s_decay + _dot(kdec_t[hd][:, prows(pb)], vn_pad[hd])
    mix_heads = []
    for hd in heads:
        state_ref[hd] = s[hd]
        mix_heads.append(_rmsnorm(jnp.concatenate(o_chunks[hd], axis=0), onw_ref[...]))
    mix = jnp.concatenate(mix_heads, axis=-1)

    attn = _mem_attention(_dot(h, wqz_ref[:, :d_xa]), kt_ref, v_ref)
    z = _dot(h, wqz_ref[:, d_xa:])
    o_ref[0] = _gate_project_residual(x, mix, attn, z, wout_ref, npost_ref[...])


def _const_spec(shape):
    zeros = (0,) * len(shape)
    return pl.BlockSpec(shape, lambda b, j: zeros, pipeline_mode=pl.Buffered(1))


def _layer_call(kernel, name, tm, x, consts, kt, v, scratch_shapes=()):
    b, s, d = x.shape
    lead, tail = consts[:-2], consts[-2:]
    operands = [x, *lead, kt, v, *tail]
    in_specs = [pl.BlockSpec((1, tm, d), lambda bi, j: (bi, j, 0))]
    in_specs += [_const_spec(c.shape) for c in lead]
    in_specs += [pl.BlockSpec((1,) + kt.shape[1:], lambda bi, j: (bi, 0, 0)),
                 pl.BlockSpec((1,) + v.shape[1:], lambda bi, j: (bi, 0, 0))]
    in_specs += [_const_spec(c.shape) for c in tail]
    return pl.pallas_call(
        kernel,
        grid=(b, s // tm),
        in_specs=in_specs,
        out_specs=pl.BlockSpec((1, tm, d), lambda bi, j: (bi, j, 0)),
        out_shape=jax.ShapeDtypeStruct(x.shape, x.dtype),
        scratch_shapes=list(scratch_shapes),
        compiler_params=pltpu.CompilerParams(
            dimension_semantics=("arbitrary", "arbitrary"),
            vmem_limit_bytes=_VMEM_LIMIT),
        name=name,
    )(*operands)


def kernel(x, mem, mem_norm_w, w_mem_kv, norm_pre, norm_post, w_out, a_w_in, a_ln_w, a_ln_b, a_w_s, a_b_s, b_w_in, b_conv_w, c_w_in, c_conv_w, c_a_log, c_dt_bias, c_o_norm_w):
    depth, d = norm_pre.shape
    n_mixers = 3
    d_xa = w_mem_kv.shape[1] // 2
    n_heads = c_a_log.shape[1]
    assert x.shape[1] % _TM_AB == 0 and x.shape[1] % _TM_C == 0
    assert d // n_heads == _HEAD and d_xa == _XA_HEADS * _HEAD

    kt, v = _memkv(mem, mem_norm_w, w_mem_kv)
    row = lambda p: p.reshape(1, -1)
    pad_lanes = lambda p: jnp.pad(p, ((0, 0), (0, _HEAD - p.shape[1])))

    for i in range(depth):
        kind, j = i % n_mixers, i // n_mixers
        wout = w_out[i].astype(_BF16)
        if kind == 0:
            bs_full = jnp.repeat(a_b_s[j].T, _HEAD, axis=1)
            consts = [row(norm_pre[i]), a_w_in[j].astype(_BF16), row(a_ln_w[j]),
                      row(a_ln_b[j]), a_w_s[j], bs_full, wout, row(norm_post[i])]
            x = _layer_call(_layer_a_kernel, "layer_gmlp", _TM_AB, x, consts, kt, v)
        elif kind == 1:
            consts = [row(norm_pre[i]), b_w_in[j].astype(_BF16), b_conv_w[j],
                      wout, row(norm_post[i])]
            x = _layer_call(_layer_b_kernel, "layer_sconv", _TM_AB, x, consts, kt, v,
                            scratch_shapes=[pltpu.VMEM((_CARRY_ROWS, d), _F32)])
        else:
            w = c_w_in[j]
            c0 = 3 * d
            w_ab = jnp.concatenate([pad_lanes(w[:, c0:c0 + n_heads]),
                                    pad_lanes(w[:, c0 + n_heads:c0 + 2 * n_heads])], axis=1)
            consts = [row(norm_pre[i]), w[:, :c0].astype(_BF16), w_ab.astype(_BF16),
                      w[:, c0 + 2 * n_heads:].astype(_BF16), c_conv_w[j],
                      pad_lanes(row(c_a_log[j])), pad_lanes(row(c_dt_bias[j])),
                      row(c_o_norm_w[j]), wout, row(norm_post[i])]
            x = _layer_call(_layer_c_kernel, "layer_gdn", _TM_C, x, consts, kt, v,
                            scratch_shapes=[pltpu.VMEM((_CARRY_ROWS, 3 * d), _F32),
                                            pltpu.VMEM((n_heads, _HEAD, _HEAD), _F32)])
    return x
```

```python
import math

import jax
import jax.numpy as jnp
from jax import lax
from jax.experimental import pallas as pl
from jax.experimental.pallas import tpu as pltpu

_F32 = jnp.float32
_BF16 = jnp.bfloat16

_EPS = 1e-6
_LOG2E = math.log2(math.e)
_XA_HEADS = 4
_HEAD = 128
_CW = 2 * _HEAD
_GM_CHUNK = 128
_DN_CHUNK = 64
_DN_PAIR = 2 * _DN_CHUNK
_DN_CONV = 4
_SC_WIDTH = 3
_CARRY_ROWS = 8

_TM_AB = 512
_TM_C = 256
_NB_C = 2
_GDN_LEAD = 10
_VMEM_LIMIT = 56 * 1024 * 1024


def _dot(a, b):
    return jnp.dot(a, b, preferred_element_type=_F32)


def _dot_nt(a, b):
    return lax.dot_general(a, b, (((1,), (1,)), ((), ())), preferred_element_type=_F32)


def _proj_chunks(h, w_ref, lo, hi):
    return [_dot(h, w_ref[:, c:c + _CW]) for c in range(lo, hi, _CW)]


def _head_cols(chunks, hd):
    off = (hd % 2) * _HEAD
    return chunks[hd // 2][:, off:off + _HEAD]


def _rmsnorm(x, w):
    ms = jnp.mean(x * x, axis=-1, keepdims=True)
    return x * lax.rsqrt(ms + _EPS) * w


def _sigmoid(x):
    return 1.0 / (1.0 + jnp.exp2(x * (-_LOG2E)))


def _silu(x):
    return x * _sigmoid(x)


def _gelu_tanh(x):
    c2 = 2.0 * math.sqrt(2.0 / math.pi)
    t = x * ((x * x) * (-c2 * 0.044715 * _LOG2E) + (-c2 * _LOG2E))
    return x * (1.0 / (1.0 + jnp.exp2(t)))


def _softplus(x):
    return jnp.maximum(x, 0.0) + jnp.log(1.0 + jnp.exp(-jnp.abs(x)))


def _causal_conv_chunk(hist_ref, cw_ref, cur, col, taps):
    tm = cur.shape[0]
    cols = pl.ds(col, _CW)
    hist_ref[pl.ds(_CARRY_ROWS, tm), cols] = cur
    acc = cur * cw_ref[taps - 1:taps, col:col + _CW]
    for k in range(taps - 1):
        delayed = hist_ref[pl.ds(_CARRY_ROWS - (taps - 1 - k), tm), cols]
        acc = acc + delayed * cw_ref[k:k + 1, col:col + _CW]
    return acc


def _roll_history(hist_ref, tm):
    hist_ref[pl.ds(0, _CARRY_ROWS), :] = hist_ref[pl.ds(tm, _CARRY_ROWS), :]


def _mem_attention(q_chunks, kt_ref, v_ref):
    scale = _HEAD ** -0.5
    heads = []
    for h in range(_XA_HEADS):
        sl = slice(h * _HEAD, (h + 1) * _HEAD)
        s = _dot(_head_cols(q_chunks, h).astype(_BF16), kt_ref[0, sl, :])
        m = jnp.max(s, axis=-1, keepdims=True)
        p = jnp.exp2((s - m) * (scale * _LOG2E))
        l = jnp.sum(p, axis=-1, keepdims=True)
        o = _dot(p.astype(_BF16), v_ref[0, :, sl])
        heads.append(o * (1.0 / l))
    return [jnp.concatenate(heads[i:i + 2], axis=-1) for i in range(0, _XA_HEADS, 2)]


def _gate_chunks(h, w_ref, lo):
    return [_silu(c) for c in _proj_chunks(h, w_ref, lo, w_ref.shape[1])]


def _gate_project_residual(x, y_chunks, gate_chunks, wout_ref, npost):
    y = jnp.concatenate([(yc * gc).astype(_BF16) for yc, gc in zip(y_chunks, gate_chunks)],
                        axis=-1)
    return x + _rmsnorm(_dot(y, wout_ref[...]), npost)


def _memkv_kernel(mem_ref, nw_ref, w_ref, kt_ref, v_ref):
    d_xa = v_ref.shape[-1]
    mn = _rmsnorm(mem_ref[0], nw_ref[...]).astype(_BF16)
    kv = _dot(mn, w_ref[...])
    kt_ref[0] = kv[:, :d_xa].T.astype(_BF16)
    v_ref[0] = kv[:, d_xa:].astype(_BF16)


def _memkv(mem, mem_norm_w, w_mem_kv):
    b, n_mem, d = mem.shape
    d_xa = w_mem_kv.shape[1] // 2
    return pl.pallas_call(
        _memkv_kernel,
        grid=(b,),
        in_specs=[
            pl.BlockSpec((1, n_mem, d), lambda i: (i, 0, 0)),
            pl.BlockSpec((1, d), lambda i: (0, 0)),
            pl.BlockSpec((d, 2 * d_xa), lambda i: (0, 0)),
        ],
        out_specs=[
            pl.BlockSpec((1, d_xa, n_mem), lambda i: (i, 0, 0)),
            pl.BlockSpec((1, n_mem, d_xa), lambda i: (i, 0, 0)),
        ],
        out_shape=[
            jax.ShapeDtypeStruct((b, d_xa, n_mem), _BF16),
            jax.ShapeDtypeStruct((b, n_mem, d_xa), _BF16),
        ],
        compiler_params=pltpu.CompilerParams(dimension_semantics=("arbitrary",)),
        name="memkv",
    )(mem, mem_norm_w.reshape(1, d), w_mem_kv.astype(_BF16))


def _layer_a_kernel(x_ref, npre_ref, win_ref, lnw_ref, lnb_ref, ws_ref, bs_ref,
                    kt_ref, v_ref, wout_ref, npost_ref, o_ref):
    x = x_ref[0]
    tm, d = x.shape
    d_xa = v_ref.shape[-1]
    h = _rmsnorm(x, npre_ref[...]).astype(_BF16)

    u = [_gelu_tanh(c) for c in _proj_chunks(h, win_ref, 0, d)]
    v = [_gelu_tanh(c) for c in _proj_chunks(h, win_ref, d, 2 * d)]
    q_chunks = _proj_chunks(h, win_ref, 2 * d, 2 * d + d_xa)

    mean = sum(jnp.sum(c, axis=-1, keepdims=True) for c in v) * (1.0 / d)
    vc = [c - mean for c in v]
    var = sum(jnp.sum(c * c, axis=-1, keepdims=True) for c in vc) * (1.0 / d)
    rstd = lax.rsqrt(var + _EPS)
    vn = [(c * rstd * lnw_ref[:, i * _CW:(i + 1) * _CW]
           + lnb_ref[:, i * _CW:(i + 1) * _CW]).astype(_BF16) for i, c in enumerate(vc)]

    n_groups = ws_ref.shape[0]
    ti = lax.broadcasted_iota(jnp.int32, (_GM_CHUNK, _GM_CHUNK), 0)
    si = lax.broadcasted_iota(jnp.int32, (_GM_CHUNK, _GM_CHUNK), 1)
    causal = ti >= si
    ws = [jnp.where(causal, ws_ref[g], 0.0).astype(_BF16) for g in range(n_groups)]
    sp = [[_dot(ws[g], _head_cols(vn, g)[n * _GM_CHUNK:(n + 1) * _GM_CHUNK, :])
           for n in range(tm // _GM_CHUNK)] for g in range(n_groups)]
    mix = []
    for i in range(d // _CW):
        bias = bs_ref[:, i * _CW:(i + 1) * _CW]
        rows = [jnp.concatenate([sp[2 * i][n], sp[2 * i + 1][n]], axis=-1) + bias
                for n in range(tm // _GM_CHUNK)]
        mix.append(u[i] * jnp.concatenate(rows, axis=0))

    attn = _mem_attention(q_chunks, kt_ref, v_ref)
    z = _gate_chunks(h, win_ref, 2 * d + d_xa)
    o_ref[0] = _gate_project_residual(x, mix + attn, z, wout_ref, npost_ref[...])


def _layer_b_kernel(x_ref, npre_ref, win_ref, cw_ref, kt_ref, v_ref, wout_ref,
                    npost_ref, o_ref, hist_ref):
    @pl.when(pl.program_id(1) == 0)
    def _():
        hist_ref[pl.ds(0, _CARRY_ROWS), :] = jnp.zeros((_CARRY_ROWS, hist_ref.shape[1]), _F32)

    x = x_ref[0]
    tm, d = x.shape
    d_xa = v_ref.shape[-1]
    h = _rmsnorm(x, npre_ref[...]).astype(_BF16)

    cg = _proj_chunks(h, win_ref, d, 2 * d)
    hv = _proj_chunks(h, win_ref, 2 * d, 3 * d)
    conv = [_causal_conv_chunk(hist_ref, cw_ref, c * hvc, i * _CW, _SC_WIDTH)
            for i, (c, hvc) in enumerate(zip(cg, hv))]
    _roll_history(hist_ref, tm)
    bg = _proj_chunks(h, win_ref, 0, d)
    mix = [b * c for b, c in zip(bg, conv)]

    attn = _mem_attention(_proj_chunks(h, win_ref, 3 * d, 3 * d + d_xa), kt_ref, v_ref)
    z = _gate_chunks(h, win_ref, 3 * d + d_xa)
    o_ref[0] = _gate_project_residual(x, mix + attn, z, wout_ref, npost_ref[...])


def _unit_lower_inverse(mats):
    n = mats[0].shape[0]
    ri = lax.broadcasted_iota(jnp.int32, (n, n), 0)
    ci = lax.broadcasted_iota(jnp.int32, (n, n), 1)
    same = ri ^ ci
    in16 = same < 16
    in32 = same < 32
    mid32 = jnp.logical_and(in32, jnp.logical_not(in16))
    eye = (ri == ci).astype(_F32)

    def mm(ps, qs):
        return [_dot(p.astype(_BF16), q.astype(_BF16)) for p, q in zip(ps, qs)]

    def add(ps, qs):
        return [p + q for p, q in zip(ps, qs)]

    def sub(ps, qs):
        return [p - q for p, q in zip(ps, qs)]

    nd = [jnp.where(in16, -a, 0.0) for a in mats]
    n2 = mm(nd, nd)
    x = [eye + m for m in nd]
    yield
    x = add(x, mm(x, n2))
    n4 = mm(n2, n2)
    yield
    x = add(x, mm(x, n4))
    n8 = mm(n4, n4)
    yield
    x = add(x, mm(x, n8))
    off32 = [jnp.where(mid32, a, 0.0) for a in mats]
    yield
    t = mm(x, off32)
    yield
    x = sub(x, mm(t, x))
    off64 = [jnp.where(in32, 0.0, a) for a in mats]
    yield
    t = mm(x, off64)
    yield
    x = sub(x, mm(t, x))
    return x


def _gdn_tile(x, npre_ref, wqkv_ref, wab_ref, wqz_ref, cw_ref, alog_ref, dtb_ref, onw_ref,
              kt_ref, v_ref, wout_ref, npost_ref, hist_ref, state_ref):
    tm, d = x.shape
    d_xa = v_ref.shape[-1]
    n_heads = state_ref.shape[0]
    h = _rmsnorm(x, npre_ref[...]).astype(_BF16)

    ab = _dot(h, wab_ref[...])
    g = -jnp.exp(alog_ref[...]) * _softplus(ab[:, :_HEAD] + dtb_ref[...])
    beta = _sigmoid(ab[:, _HEAD:])
    t_in = lax.broadcasted_iota(jnp.int32, (tm, 1), 0) & (_DN_CHUNK - 1)
    gc = g
    step = 1
    while step < _DN_CHUNK:
        gc = gc + jnp.where(t_in >= step, pltpu.roll(gc, step, axis=0), 0.0)
        step *= 2
    gc3 = gc.reshape(tm // _DN_CHUNK, _DN_CHUNK, _HEAD)
    g_end = jnp.broadcast_to(gc3[:, _DN_CHUNK - 1:, :], gc3.shape).reshape(tm, _HEAD)
    gc_t = gc.T
    e_gc = jnp.exp(gc)
    e_rest = jnp.exp(g_end - gc)
    e_end = jnp.exp(g_end)
    yield

    qkv = []
    for i in range(3 * d // _CW):
        pre = _dot(h, wqkv_ref[:, i * _CW:(i + 1) * _CW])
        qkv.append(_silu(_causal_conv_chunk(hist_ref, cw_ref, pre, i * _CW, _DN_CONV)))
        yield
    _roll_history(hist_ref, tm)
    nq = d // _CW
    q_chunks, k_chunks, v_chunks = qkv[:nq], qkv[nq:2 * nq], qkv[2 * nq:]
    q_att = _proj_chunks(h, wqz_ref, 0, d_xa)
    yield
    z = []
    for c in range(d_xa, wqz_ref.shape[1], _CW):
        z.append(_silu(_dot(h, wqz_ref[:, c:c + _CW])))
        yield

    ri = lax.broadcasted_iota(jnp.int32, (_DN_PAIR, _DN_PAIR), 0)
    ci = lax.broadcasted_iota(jnp.int32, (_DN_PAIR, _DN_PAIR), 1)
    incl = jnp.logical_and(ri >= ci, (ri ^ ci) < _DN_CHUNK)
    diag = ri == ci
    zeros_half = jnp.zeros((_DN_CHUNK, _HEAD), _F32)
    q_scale = _HEAD ** -0.5

    heads = range(n_heads)
    pairs = range(tm // _DN_PAIR)
    k_bf, kb_bf, q_bf, rhs_uw, q_dec, kdec_t, e_end_h = [], [], [], [], [], [], []
    for hd in heads:
        q = _head_cols(q_chunks, hd)
        k = _head_cols(k_chunks, hd)
        v = _head_cols(v_chunks, hd)
        q = q * (lax.rsqrt(jnp.sum(q * q, axis=-1, keepdims=True) + _EPS) * q_scale)
        k = k * lax.rsqrt(jnp.sum(k * k, axis=-1, keepdims=True) + _EPS)
        beta_h = beta[:, hd:hd + 1]
        e_gc_h = e_gc[:, hd:hd + 1]
        kb = k * beta_h
        k_bf.append(k.astype(_BF16))
        kb_bf.append(kb.astype(_BF16))
        q_bf.append(q.astype(_BF16))
        rhs_uw.append(jnp.concatenate([v * beta_h, kb * e_gc_h], axis=-1).astype(_BF16))
        q_dec.append(q * e_gc_h)
        kdec_t.append((k * e_rest[:, hd:hd + 1]).T.astype(_BF16))
        e_end_h.append(e_end[:, hd:hd + 1])
        yield

    systems = [(hd, pb) for hd in heads for pb in pairs]
    prows = lambda pb: slice(pb * _DN_PAIR, (pb + 1) * _DN_PAIR)
    decay = [jnp.where(incl, jnp.exp(gc[prows(pb), hd:hd + 1] - gc_t[hd:hd + 1, prows(pb)]), 0.0)
             for hd, pb in systems]
    yield
    kq = [_dot_nt(jnp.concatenate([kb_bf[hd][prows(pb), :], q_bf[hd][prows(pb), :]], axis=0),
                  k_bf[hd][prows(pb), :]) for hd, pb in systems]
    yield
    t_inv = yield from _unit_lower_inverse(
        [jnp.where(diag, 0.0, m[:_DN_PAIR, :] * dc) for m, dc in zip(kq, decay)])
    yield
    uw = {sys: _dot(t.astype(_BF16), rhs_uw[sys[0]][prows(sys[1]), :])
          for sys, t in zip(systems, t_inv)}
    qk = {sys: (m[_DN_PAIR:, :] * dc).astype(_BF16) for sys, m, dc in zip(systems, kq, decay)}
    yield

    s = [state_ref[hd] for hd in heads]
    o_chunks = [[] for _ in heads]
    for c in range(tm // _DN_CHUNK):
        pb, half = divmod(c, 2)
        lo = half * _DN_CHUNK
        rs = slice(c * _DN_CHUNK, (c + 1) * _DN_CHUNK)
        ws = [_dot(jnp.concatenate([uw[hd, pb][lo:lo + _DN_CHUNK, _HEAD:], q_dec[hd][rs, :]],
                                   axis=0).astype(_BF16), s[hd].astype(_BF16)) for hd in heads]
        yield
        vn_pad = []
        for hd in heads:
            halves = [zeros_half, zeros_half]
            halves[half] = uw[hd, pb][lo:lo + _DN_CHUNK, :_HEAD] - ws[hd][:_DN_CHUNK, :]
            vn_pad.append(jnp.concatenate(halves, axis=0).astype(_BF16))
        for hd in heads:
            o_chunks[hd].append(ws[hd][_DN_CHUNK:, :]
                                + _dot(qk[hd, pb][lo:lo + _DN_CHUNK, :], vn_pad[hd]))
        yield
        for hd in heads:
            s_decay = jnp.concatenate([e_end_h[hd][rs, :], e_end_h[hd][rs, :]], axis=0)
            s[hd] = s[hd] * s_decay + _dot(kdec_t[hd][:, prows(pb)], vn_pad[hd])
        yield
    o_heads = []
    for hd in heads:
        state_ref[hd] = s[hd]
        o_heads.append(_rmsnorm(jnp.concatenate(o_chunks[hd], axis=0), onw_ref[...]))
    mix = [jnp.concatenate(o_heads[i:i + 2], axis=-1) for i in range(0, n_heads, 2)]
    yield

    attn = _mem_attention(q_att, kt_ref, v_ref)
    yield
    return _gate_project_residual(x, mix + attn, z, wout_ref, npost_ref[...])


def _run_interleaved(gens, lead):
    results = [None] * len(gens)
    live = set(range(len(gens)))
    tick = 0
    while live:
        for i, gen in enumerate(gens):
            if i in live and tick >= i * lead:
                try:
                    next(gen)
                except StopIteration as stop:
                    results[i] = stop.value
                    live.discard(i)
        tick += 1
    return results


def _layer_c_kernel(x_ref, npre_ref, wqkv_ref, wab_ref, wqz_ref, cw_ref, alog_ref,
                    dtb_ref, onw_ref, kt_ref, v_ref, wout_ref, npost_ref, o_ref,
                    hist_ref, state_ref):
    @pl.when(pl.program_id(1) == 0)
    def _():
        hist_ref[:, pl.ds(0, _CARRY_ROWS), :] = jnp.zeros(
            (hist_ref.shape[0], _CARRY_ROWS, hist_ref.shape[2]), _F32)
        state_ref[...] = jnp.zeros_like(state_ref)

    nb = x_ref.shape[0]
    tiles = [_gdn_tile(x_ref[b], npre_ref, wqkv_ref, wab_ref, wqz_ref, cw_ref, alog_ref,
                       dtb_ref, onw_ref, kt_ref.at[pl.ds(b, 1)], v_ref.at[pl.ds(b, 1)],
                       wout_ref, npost_ref, hist_ref.at[b], state_ref.at[b])
             for b in range(nb)]
    for b, out in enumerate(_run_interleaved(tiles, _GDN_LEAD)):
        o_ref[b] = out


def _const_spec(shape):
    zeros = (0,) * len(shape)
    return pl.BlockSpec(shape, lambda b, j: zeros, pipeline_mode=pl.Buffered(1))


def _layer_call(kernel, name, tm, x, consts, kt, v, scratch_shapes=(), nb=1):
    b, s, d = x.shape
    lead, tail = consts[:-2], consts[-2:]
    operands = [x, *lead, kt, v, *tail]
    in_specs = [pl.BlockSpec((nb, tm, d), lambda bi, j: (bi, j, 0))]
    in_specs += [_const_spec(c.shape) for c in lead]
    in_specs += [pl.BlockSpec((nb,) + kt.shape[1:], lambda bi, j: (bi, 0, 0)),
                 pl.BlockSpec((nb,) + v.shape[1:], lambda bi, j: (bi, 0, 0))]
    in_specs += [_const_spec(c.shape) for c in tail]
    return pl.pallas_call(
        kernel,
        grid=(b // nb, s // tm),
        in_specs=in_specs,
        out_specs=pl.BlockSpec((nb, tm, d), lambda bi, j: (bi, j, 0)),
        out_shape=jax.ShapeDtypeStruct(x.shape, x.dtype),
        scratch_shapes=list(scratch_shapes),
        compiler_params=pltpu.CompilerParams(
            dimension_semantics=("arbitrary", "arbitrary"),
            vmem_limit_bytes=_VMEM_LIMIT),
        name=name,
    )(*operands)


def kernel(x, mem, mem_norm_w, w_mem_kv, norm_pre, norm_post, w_out, a_w_in, a_ln_w, a_ln_b, a_w_s, a_b_s, b_w_in, b_conv_w, c_w_in, c_conv_w, c_a_log, c_dt_bias, c_o_norm_w):
    depth, d = norm_pre.shape
    n_mixers = 3
    d_xa = w_mem_kv.shape[1] // 2
    n_heads = c_a_log.shape[1]
    assert x.shape[1] % _TM_AB == 0 and x.shape[1] % _TM_C == 0 and x.shape[0] % _NB_C == 0
    assert d // n_heads == _HEAD and d_xa == _XA_HEADS * _HEAD

    kt, v = _memkv(mem, mem_norm_w, w_mem_kv)
    row = lambda p: p.reshape(1, -1)
    pad_lanes = lambda p: jnp.pad(p, ((0, 0), (0, _HEAD - p.shape[1])))

    for i in range(depth):
        kind, j = i % n_mixers, i // n_mixers
        wout = w_out[i].astype(_BF16)
        if kind == 0:
            bs_full = jnp.repeat(a_b_s[j].T, _HEAD, axis=1)
            consts = [row(norm_pre[i]), a_w_in[j].astype(_BF16), row(a_ln_w[j]),
                      row(a_ln_b[j]), a_w_s[j], bs_full, wout, row(norm_post[i])]
            x = _layer_call(_layer_a_kernel, "layer_gmlp", _TM_AB, x, consts, kt, v)
        elif kind == 1:
            consts = [row(norm_pre[i]), b_w_in[j].astype(_BF16), b_conv_w[j],
                      wout, row(norm_post[i])]
            x = _layer_call(_layer_b_kernel, "layer_sconv", _TM_AB, x, consts, kt, v,
                            scratch_shapes=[pltpu.VMEM((_CARRY_ROWS + _TM_AB, d), _F32)])
        else:
            w = c_w_in[j]
            c0 = 3 * d
            w_ab = jnp.concatenate([pad_lanes(w[:, c0:c0 + n_heads]),
                                    pad_lanes(w[:, c0 + n_heads:c0 + 2 * n_heads])], axis=1)
            consts = [row(norm_pre[i]), w[:, :c0].astype(_BF16), w_ab.astype(_BF16),
                      w[:, c0 + 2 * n_heads:].astype(_BF16), c_conv_w[j],
                      pad_lanes(row(c_a_log[j])), pad_lanes(row(c_dt_bias[j])),
                      row(c_o_norm_w[j]), wout, row(norm_post[i])]
            x = _layer_call(_layer_c_kernel, "layer_gdn", _TM_C, x, consts, kt, v, nb=_NB_C,
                            scratch_shapes=[pltpu.VMEM((_NB_C, _CARRY_ROWS + _TM_C, 3 * d), _F32),
                                            pltpu.VMEM((_NB_C, n_heads, _HEAD, _HEAD), _F32)])
    return x
```

```python
import math

import jax
import jax.numpy as jnp
from jax import lax
from jax.experimental import pallas as pl
from jax.experimental.pallas import tpu as pltpu

_F32 = jnp.float32
_BF16 = jnp.bfloat16

_EPS = 1e-6
_LOG2E = math.log2(math.e)
_XA_HEADS = 4
_HEAD = 128
_CW = 2 * _HEAD
_GM_CHUNK = 128
_DN_CHUNK = 64
_DN_PAIR = 2 * _DN_CHUNK
_DN_CONV = 4
_SC_WIDTH = 3
_CARRY_ROWS = 8

_GMLP_TILE = (1024, 1)
_SCONV_TILE = (1024, 1)
_GDN_TILE = (256, 2)
_GMLP_LEAD = 12
_SCONV_LEAD = 10
_GDN_LEAD = 10
_VMEM_LIMIT = 56 * 1024 * 1024


def _dot(a, b):
    return jnp.dot(a, b, preferred_element_type=_F32)


def _dot_nt(a, b):
    return lax.dot_general(a, b, (((1,), (1,)), ((), ())), preferred_element_type=_F32)


def _head_cols(chunks, hd):
    off = (hd % 2) * _HEAD
    return chunks[hd // 2][:, off:off + _HEAD]


def _rmsnorm(x, w):
    ms = jnp.mean(x * x, axis=-1, keepdims=True)
    return x * lax.rsqrt(ms + _EPS) * w


def _sigmoid(x):
    return 1.0 / (1.0 + jnp.exp2(x * (-_LOG2E)))


def _silu(x):
    return x * _sigmoid(x)


def _gelu_tanh(x):
    c2 = 2.0 * math.sqrt(2.0 / math.pi)
    t = x * ((x * x) * (-c2 * 0.044715 * _LOG2E) + (-c2 * _LOG2E))
    return x * (1.0 / (1.0 + jnp.exp2(t)))


def _softplus(x):
    return jnp.maximum(x, 0.0) + jnp.log(1.0 + jnp.exp(-jnp.abs(x)))


def _causal_conv_chunk(hist_ref, cw_ref, cur, col, taps):
    tm = cur.shape[0]
    cols = pl.ds(col, _CW)
    hist_ref[pl.ds(_CARRY_ROWS, tm), cols] = cur
    acc = cur * cw_ref[taps - 1:taps, col:col + _CW]
    for k in range(taps - 1):
        delayed = hist_ref[pl.ds(_CARRY_ROWS - (taps - 1 - k), tm), cols]
        acc = acc + delayed * cw_ref[k:k + 1, col:col + _CW]
    return acc


def _roll_history(hist_ref, tm):
    hist_ref[pl.ds(0, _CARRY_ROWS), :] = hist_ref[pl.ds(tm, _CARRY_ROWS), :]


def _mem_attention(q_chunks, kt_ref, v_ref):
    scale = _HEAD ** -0.5
    heads = []
    for h in range(_XA_HEADS):
        sl = slice(h * _HEAD, (h + 1) * _HEAD)
        s = _dot(_head_cols(q_chunks, h).astype(_BF16), kt_ref[0, sl, :])
        m = jnp.max(s, axis=-1, keepdims=True)
        p = jnp.exp2((s - m) * (scale * _LOG2E))
        l = jnp.sum(p, axis=-1, keepdims=True)
        o = _dot(p.astype(_BF16), v_ref[0, :, sl])
        heads.append(o * (1.0 / l))
        yield
    return [jnp.concatenate(heads[i:i + 2], axis=-1) for i in range(0, _XA_HEADS, 2)]


def _proj_steps(h, w_ref, lo, hi, fn=None):
    out = []
    for c in range(lo, hi, _CW):
        chunk = _dot(h, w_ref[:, c:c + _CW])
        out.append(chunk if fn is None else fn(chunk))
        yield
    return out


def _run_interleaved(gens, lead):
    results = [None] * len(gens)
    live = set(range(len(gens)))
    tick = 0
    while live:
        for i, gen in enumerate(gens):
            if i in live and tick >= i * lead:
                try:
                    next(gen)
                except StopIteration as stop:
                    results[i] = stop.value
                    live.discard(i)
        tick += 1
    return results


def _gate_project_residual(x, y_chunks, gate_chunks, wout_ref, npost):
    y = jnp.concatenate([(yc * gc).astype(_BF16) for yc, gc in zip(y_chunks, gate_chunks)],
                        axis=-1)
    return x + _rmsnorm(_dot(y, wout_ref[...]), npost)


def _memkv_kernel(mem_ref, nw_ref, w_ref, kt_ref, v_ref):
    d_xa = v_ref.shape[-1]
    mn = _rmsnorm(mem_ref[0], nw_ref[...]).astype(_BF16)
    kv = _dot(mn, w_ref[...])
    kt_ref[0] = kv[:, :d_xa].T.astype(_BF16)
    v_ref[0] = kv[:, d_xa:].astype(_BF16)


def _memkv(mem, mem_norm_w, w_mem_kv):
    b, n_mem, d = mem.shape
    d_xa = w_mem_kv.shape[1] // 2
    return pl.pallas_call(
        _memkv_kernel,
        grid=(b,),
        in_specs=[
            pl.BlockSpec((1, n_mem, d), lambda i: (i, 0, 0)),
            pl.BlockSpec((1, d), lambda i: (0, 0)),
            pl.BlockSpec((d, 2 * d_xa), lambda i: (0, 0)),
        ],
        out_specs=[
            pl.BlockSpec((1, d_xa, n_mem), lambda i: (i, 0, 0)),
            pl.BlockSpec((1, n_mem, d_xa), lambda i: (i, 0, 0)),
        ],
        out_shape=[
            jax.ShapeDtypeStruct((b, d_xa, n_mem), _BF16),
            jax.ShapeDtypeStruct((b, n_mem, d_xa), _BF16),
        ],
        compiler_params=pltpu.CompilerParams(dimension_semantics=("arbitrary",)),
        name="memkv",
    )(mem, mem_norm_w.reshape(1, d), w_mem_kv.astype(_BF16))


def _gmlp_tile(x, ws, npre_ref, win_ref, lnw_ref, lnb_ref, bs_ref, kt_ref, v_ref, wout_ref,
               npost_ref):
    tm, d = x.shape
    d_xa = v_ref.shape[-1]
    h = _rmsnorm(x, npre_ref[...]).astype(_BF16)
    yield
    u = yield from _proj_steps(h, win_ref, 0, d, _gelu_tanh)
    v = yield from _proj_steps(h, win_ref, d, 2 * d, _gelu_tanh)
    q_chunks = yield from _proj_steps(h, win_ref, 2 * d, 2 * d + d_xa)

    mean = sum(jnp.sum(c, axis=-1, keepdims=True) for c in v) * (1.0 / d)
    vc = [c - mean for c in v]
    var = sum(jnp.sum(c * c, axis=-1, keepdims=True) for c in vc) * (1.0 / d)
    rstd = lax.rsqrt(var + _EPS)
    vn = [(c * rstd * lnw_ref[:, i * _CW:(i + 1) * _CW]
           + lnb_ref[:, i * _CW:(i + 1) * _CW]).astype(_BF16) for i, c in enumerate(vc)]
    yield

    mix = []
    for i in range(d // _CW):
        bias = bs_ref[:, i * _CW:(i + 1) * _CW]
        rows = []
        for n in range(tm // _GM_CHUNK):
            rs = slice(n * _GM_CHUNK, (n + 1) * _GM_CHUNK)
            rows.append(jnp.concatenate([_dot(ws[g], _head_cols(vn, g)[rs, :])
                                         for g in (2 * i, 2 * i + 1)], axis=-1) + bias)
        mix.append(u[i] * jnp.concatenate(rows, axis=0))
        yield

    attn = yield from _mem_attention(q_chunks, kt_ref, v_ref)
    z = yield from _proj_steps(h, win_ref, 2 * d + d_xa, win_ref.shape[1], _silu)
    return _gate_project_residual(x, mix + attn, z, wout_ref, npost_ref[...])


def _layer_a_kernel(x_ref, npre_ref, win_ref, lnw_ref, lnb_ref, ws_ref, bs_ref,
                    kt_ref, v_ref, wout_ref, npost_ref, o_ref):
    ti = lax.broadcasted_iota(jnp.int32, (_GM_CHUNK, _GM_CHUNK), 0)
    si = lax.broadcasted_iota(jnp.int32, (_GM_CHUNK, _GM_CHUNK), 1)
    ws = [jnp.where(ti >= si, ws_ref[g], 0.0).astype(_BF16) for g in range(ws_ref.shape[0])]
    tiles = [_gmlp_tile(x_ref[b], ws, npre_ref, win_ref, lnw_ref, lnb_ref, bs_ref,
                        kt_ref.at[pl.ds(b, 1)], v_ref.at[pl.ds(b, 1)], wout_ref, npost_ref)
             for b in range(x_ref.shape[0])]
    for b, out in enumerate(_run_interleaved(tiles, _GMLP_LEAD)):
        o_ref[b] = out


def _sconv_tile(x, npre_ref, win_ref, cw_ref, kt_ref, v_ref, wout_ref, npost_ref, hist_ref):
    tm, d = x.shape
    d_xa = v_ref.shape[-1]
    h = _rmsnorm(x, npre_ref[...]).astype(_BF16)
    yield
    conv = []
    for i in range(d // _CW):
        c0 = i * _CW
        p = _dot(h, win_ref[:, d + c0:d + c0 + _CW]) * _dot(h, win_ref[:, 2 * d + c0:2 * d + c0 + _CW])
        conv.append(_causal_conv_chunk(hist_ref, cw_ref, p, c0, _SC_WIDTH))
        yield
    _roll_history(hist_ref, tm)
    bg = yield from _proj_steps(h, win_ref, 0, d)
    mix = [b * c for b, c in zip(bg, conv)]
    q_chunks = yield from _proj_steps(h, win_ref, 3 * d, 3 * d + d_xa)
    attn = yield from _mem_attention(q_chunks, kt_ref, v_ref)
    z = yield from _proj_steps(h, win_ref, 3 * d + d_xa, win_ref.shape[1], _silu)
    return _gate_project_residual(x, mix + attn, z, wout_ref, npost_ref[...])


def _layer_b_kernel(x_ref, npre_ref, win_ref, cw_ref, kt_ref, v_ref, wout_ref,
                    npost_ref, o_ref, hist_ref):
    @pl.when(pl.program_id(1) == 0)
    def _():
        hist_ref[:, pl.ds(0, _CARRY_ROWS), :] = jnp.zeros(
            (hist_ref.shape[0], _CARRY_ROWS, hist_ref.shape[2]), _F32)

    tiles = [_sconv_tile(x_ref[b], npre_ref, win_ref, cw_ref, kt_ref.at[pl.ds(b, 1)],
                         v_ref.at[pl.ds(b, 1)], wout_ref, npost_ref, hist_ref.at[b])
             for b in range(x_ref.shape[0])]
    for b, out in enumerate(_run_interleaved(tiles, _SCONV_LEAD)):
        o_ref[b] = out


def _unit_lower_inverse(mats):
    n = mats[0].shape[0]
    ri = lax.broadcasted_iota(jnp.int32, (n, n), 0)
    ci = lax.broadcasted_iota(jnp.int32, (n, n), 1)
    same = ri ^ ci
    in16 = same < 16
    in32 = same < 32
    mid32 = jnp.logical_and(in32, jnp.logical_not(in16))
    eye = (ri == ci).astype(_F32)

    def mm(ps, qs):
        return [_dot(p.astype(_BF16), q.astype(_BF16)) for p, q in zip(ps, qs)]

    def add(ps, qs):
        return [p + q for p, q in zip(ps, qs)]

    def sub(ps, qs):
        return [p - q for p, q in zip(ps, qs)]

    nd = [jnp.where(in16, -a, 0.0) for a in mats]
    n2 = mm(nd, nd)
    x = [eye + m for m in nd]
    yield
    x = add(x, mm(x, n2))
    n4 = mm(n2, n2)
    yield
    x = add(x, mm(x, n4))
    n8 = mm(n4, n4)
    yield
    x = add(x, mm(x, n8))
    off32 = [jnp.where(mid32, a, 0.0) for a in mats]
    yield
    t = mm(x, off32)
    yield
    x = sub(x, mm(t, x))
    off64 = [jnp.where(in32, 0.0, a) for a in mats]
    yield
    t = mm(x, off64)
    yield
    x = sub(x, mm(t, x))
    return x


def _gdn_tile(x, npre_ref, wqkv_ref, wab_ref, wqz_ref, cw_ref, alog_ref, dtb_ref, onw_ref,
              kt_ref, v_ref, wout_ref, npost_ref, hist_ref, state_ref):
    tm, d = x.shape
    d_xa = v_ref.shape[-1]
    n_heads = state_ref.shape[0]
    h = _rmsnorm(x, npre_ref[...]).astype(_BF16)

    ab = _dot(h, wab_ref[...])
    g = -jnp.exp(alog_ref[...]) * _softplus(ab[:, :_HEAD] + dtb_ref[...])
    beta = _sigmoid(ab[:, _HEAD:])
    t_in = lax.broadcasted_iota(jnp.int32, (tm, 1), 0) & (_DN_CHUNK - 1)
    gc = g
    step = 1
    while step < _DN_CHUNK:
        gc = gc + jnp.where(t_in >= step, pltpu.roll(gc, step, axis=0), 0.0)
        step *= 2
    gc3 = gc.reshape(tm // _DN_CHUNK, _DN_CHUNK, _HEAD)
    g_end = jnp.broadcast_to(gc3[:, _DN_CHUNK - 1:, :], gc3.shape).reshape(tm, _HEAD)
    gc_t = gc.T
    e_gc = jnp.exp(gc)
    e_rest = jnp.exp(g_end - gc)
    e_end = jnp.exp(g_end)
    yield

    qkv = []
    for i in range(3 * d // _CW):
        pre = _dot(h, wqkv_ref[:, i * _CW:(i + 1) * _CW])
        qkv.append(_silu(_causal_conv_chunk(hist_ref, cw_ref, pre, i * _CW, _DN_CONV)))
        yield
    _roll_history(hist_ref, tm)
    nq = d // _CW
    q_chunks, k_chunks, v_chunks = qkv[:nq], qkv[nq:2 * nq], qkv[2 * nq:]
    q_att = yield from _proj_steps(h, wqz_ref, 0, d_xa)
    z = yield from _proj_steps(h, wqz_ref, d_xa, wqz_ref.shape[1], _silu)

    ri = lax.broadcasted_iota(jnp.int32, (_DN_PAIR, _DN_PAIR), 0)
    ci = lax.broadcasted_iota(jnp.int32, (_DN_PAIR, _DN_PAIR), 1)
    incl = jnp.logical_and(ri >= ci, (ri ^ ci) < _DN_CHUNK)
    diag = ri == ci
    zeros_half = jnp.zeros((_DN_CHUNK, _HEAD), _F32)
    q_scale = _HEAD ** -0.5

    heads = range(n_heads)
    pairs = range(tm // _DN_PAIR)
    k_bf, kb_bf, q_bf, rhs_uw, q_dec, kdec_t, e_end_h = [], [], [], [], [], [], []
    for hd in heads:
        q = _head_cols(q_chunks, hd)
        k = _head_cols(k_chunks, hd)
        v = _head_cols(v_chunks, hd)
        q = q * (lax.rsqrt(jnp.sum(q * q, axis=-1, keepdims=True) + _EPS) * q_scale)
        k = k * lax.rsqrt(jnp.sum(k * k, axis=-1, keepdims=True) + _EPS)
        beta_h = beta[:, hd:hd + 1]
        e_gc_h = e_gc[:, hd:hd + 1]
        kb = k * beta_h
        k_bf.append(k.astype(_BF16))
        kb_bf.append(kb.astype(_BF16))
        q_bf.append(q.astype(_BF16))
        rhs_uw.append(jnp.concatenate([v * beta_h, kb * e_gc_h], axis=-1).astype(_BF16))
        q_dec.append(q * e_gc_h)
        kdec_t.append((k * e_rest[:, hd:hd + 1]).T.astype(_BF16))
        e_end_h.append(e_end[:, hd:hd + 1])
        yield

    systems = [(hd, pb) for hd in heads for pb in pairs]
    prows = lambda pb: slice(pb * _DN_PAIR, (pb + 1) * _DN_PAIR)
    decay = [jnp.where(incl, jnp.exp(gc[prows(pb), hd:hd + 1] - gc_t[hd:hd + 1, prows(pb)]), 0.0)
             for hd, pb in systems]
    yield
    kq = [_dot_nt(jnp.concatenate([kb_bf[hd][prows(pb), :], q_bf[hd][prows(pb), :]], axis=0),
                  k_bf[hd][prows(pb), :]) for hd, pb in systems]
    yield
    t_inv = yield from _unit_lower_inverse(
        [jnp.where(diag, 0.0, m[:_DN_PAIR, :] * dc) for m, dc in zip(kq, decay)])
    yield
    uw = {sys: _dot(t.astype(_BF16), rhs_uw[sys[0]][prows(sys[1]), :])
          for sys, t in zip(systems, t_inv)}
    qk = {sys: (m[_DN_PAIR:, :] * dc).astype(_BF16) for sys, m, dc in zip(systems, kq, decay)}
    yield

    s = [state_ref[hd] for hd in heads]
    o_chunks = [[] for _ in heads]
    for c in range(tm // _DN_CHUNK):
        pb, half = divmod(c, 2)
        lo = half * _DN_CHUNK
        rs = slice(c * _DN_CHUNK, (c + 1) * _DN_CHUNK)
        ws = [_dot(jnp.concatenate([uw[hd, pb][lo:lo + _DN_CHUNK, _HEAD:], q_dec[hd][rs, :]],
                                   axis=0).astype(_BF16), s[hd].astype(_BF16)) for hd in heads]
        yield
        vn_pad = []
        for hd in heads:
            halves = [zeros_half, zeros_half]
            halves[half] = uw[hd, pb][lo:lo + _DN_CHUNK, :_HEAD] - ws[hd][:_DN_CHUNK, :]
            vn_pad.append(jnp.concatenate(halves, axis=0).astype(_BF16))
        for hd in heads:
            o_chunks[hd].append(ws[hd][_DN_CHUNK:, :]
                                + _dot(qk[hd, pb][lo:lo + _DN_CHUNK, :], vn_pad[hd]))
        yield
        for hd in heads:
            s_decay = jnp.concatenate([e_end_h[hd][rs, :], e_end_h[hd][rs, :]], axis=0)
            s[hd] = s[hd] * s_decay + _dot(kdec_t[hd][:, prows(pb)], vn_pad[hd])
        yield
    o_heads = []
    for hd in heads:
        state_ref[hd] = s[hd]
        o_heads.append(_rmsnorm(jnp.concatenate(o_chunks[hd], axis=0), onw_ref[...]))
    mix = [jnp.concatenate(o_heads[i:i + 2], axis=-1) for i in range(0, n_heads, 2)]
    yield

    attn = yield from _mem_attention(q_att, kt_ref, v_ref)
    return _gate_project_residual(x, mix + attn, z, wout_ref, npost_ref[...])


def _layer_c_kernel(x_ref, npre_ref, wqkv_ref, wab_ref, wqz_ref, cw_ref, alog_ref,
                    dtb_ref, onw_ref, kt_ref, v_ref, wout_ref, npost_ref, o_ref,
                    hist_ref, state_ref):
    @pl.when(pl.program_id(1) == 0)
    def _():
        hist_ref[:, pl.ds(0, _CARRY_ROWS), :] = jnp.zeros(
            (hist_ref.shape[0], _CARRY_ROWS, hist_ref.shape[2]), _F32)
        state_ref[...] = jnp.zeros_like(state_ref)

    nb = x_ref.shape[0]
    tiles = [_gdn_tile(x_ref[b], npre_ref, wqkv_ref, wab_ref, wqz_ref, cw_ref, alog_ref,
                       dtb_ref, onw_ref, kt_ref.at[pl.ds(b, 1)], v_ref.at[pl.ds(b, 1)],
                       wout_ref, npost_ref, hist_ref.at[b], state_ref.at[b])
             for b in range(nb)]
    for b, out in enumerate(_run_interleaved(tiles, _GDN_LEAD)):
        o_ref[b] = out


def _const_spec(operand):
    if isinstance(operand, tuple):
        stacked, layer = operand
        zeros = (0,) * (stacked.ndim - 1)
        return pl.BlockSpec((None,) + stacked.shape[1:], lambda b, j: (layer,) + zeros,
                            pipeline_mode=pl.Buffered(1))
    zeros = (0,) * operand.ndim
    return pl.BlockSpec(operand.shape, lambda b, j: zeros, pipeline_mode=pl.Buffered(1))


def _layer_call(kernel, name, tm, x, consts, kt, v, scratch_shapes=(), nb=1):
    b, s, d = x.shape
    lead, tail = consts[:-2], consts[-2:]
    operands = [x, *lead, kt, v, *tail]
    operands = [op[0] if isinstance(op, tuple) else op for op in operands]
    in_specs = [pl.BlockSpec((nb, tm, d), lambda bi, j: (bi, j, 0))]
    in_specs += [_const_spec(c) for c in lead]
    in_specs += [pl.BlockSpec((nb,) + kt.shape[1:], lambda bi, j: (bi, 0, 0)),
                 pl.BlockSpec((nb,) + v.shape[1:], lambda bi, j: (bi, 0, 0))]
    in_specs += [_const_spec(c) for c in tail]
    return pl.pallas_call(
        kernel,
        grid=(b // nb, s // tm),
        in_specs=in_specs,
        out_specs=pl.BlockSpec((nb, tm, d), lambda bi, j: (bi, j, 0)),
        out_shape=jax.ShapeDtypeStruct(x.shape, x.dtype),
        scratch_shapes=list(scratch_shapes),
        compiler_params=pltpu.CompilerParams(
            dimension_semantics=("arbitrary", "arbitrary"),
            vmem_limit_bytes=_VMEM_LIMIT),
        name=name,
    )(*operands)


def kernel(x, mem, mem_norm_w, w_mem_kv, norm_pre, norm_post, w_out, a_w_in, a_ln_w, a_ln_b, a_w_s, a_b_s, b_w_in, b_conv_w, c_w_in, c_conv_w, c_a_log, c_dt_bias, c_o_norm_w):
    depth, d = norm_pre.shape
    n_mixers = 3
    d_xa = w_mem_kv.shape[1] // 2
    n_heads = c_a_log.shape[1]
    for tm, nb in (_GMLP_TILE, _SCONV_TILE, _GDN_TILE):
        assert x.shape[1] % tm == 0 and x.shape[0] % nb == 0
    assert d // n_heads == _HEAD and d_xa == _XA_HEADS * _HEAD

    kt, v = _memkv(mem, mem_norm_w, w_mem_kv)
    row = lambda p: p.reshape(1, -1)
    pad_lanes = lambda p: jnp.pad(p, ((0, 0), (0, _HEAD - p.shape[1])))

    w_out_bf = w_out.astype(_BF16)
    a_w_in_bf = a_w_in.astype(_BF16)
    b_w_in_bf = b_w_in.astype(_BF16)
    for i in range(depth):
        kind, j = i % n_mixers, i // n_mixers
        wout = (w_out_bf, i)
        if kind == 0:
            bs_full = jnp.repeat(a_b_s[j].T, _HEAD, axis=1)
            consts = [row(norm_pre[i]), (a_w_in_bf, j), row(a_ln_w[j]),
                      row(a_ln_b[j]), (a_w_s, j), bs_full, wout, row(norm_post[i])]
            tm, nb = _GMLP_TILE
            x = _layer_call(_layer_a_kernel, "layer_gmlp", tm, x, consts, kt, v, nb=nb)
        elif kind == 1:
            consts = [row(norm_pre[i]), (b_w_in_bf, j), (b_conv_w, j),
                      wout, row(norm_post[i])]
            tm, nb = _SCONV_TILE
            x = _layer_call(_layer_b_kernel, "layer_sconv", tm, x, consts, kt, v, nb=nb,
                            scratch_shapes=[pltpu.VMEM((nb, _CARRY_ROWS + tm, d), _F32)])
        else:
            w = c_w_in[j]
            c0 = 3 * d
            w_ab = jnp.concatenate([pad_lanes(w[:, c0:c0 + n_heads]),
                                    pad_lanes(w[:, c0 + n_heads:c0 + 2 * n_heads])], axis=1)
            consts = [row(norm_pre[i]), w[:, :c0].astype(_BF16), w_ab.astype(_BF16),
                      w[:, c0 + 2 * n_heads:].astype(_BF16), (c_conv_w, j),
                      pad_lanes(row(c_a_log[j])), pad_lanes(row(c_dt_bias[j])),
                      row(c_o_norm_w[j]), wout, row(norm_post[i])]
            tm, nb = _GDN_TILE
            x = _layer_call(_layer_c_kernel, "layer_gdn", tm, x, consts, kt, v, nb=nb,
                            scratch_shapes=[pltpu.VMEM((nb, _CARRY_ROWS + tm, 3 * d), _F32),
                                            pltpu.VMEM((nb, n_heads, _HEAD, _HEAD), _F32)])
    return x
```
